```python
import jax, jax.numpy as jnp
from jax import lax
import numpy as np

D_MODEL = 1024
BATCH = 8
SEQ = 2048
DEPTH = 4
DEC_BATCH = 128
DEC_SEQ = 4
PAST_LEN = 16384
PAGE_SIZE = 128

D_A = D_MODEL
D_B = D_MODEL
D_C = D_MODEL
CONV_A_WIDTH = 31
CONV_C_WIDTH = 3
CHUNK = 128
N_GROUPS_B = 8
D_GROUP_B = D_B // N_GROUPS_B
N_BRANCH = 3
RMS_EPS = 1e-6
LN_EPS = 1e-5
WIDTHS = (D_A, D_A, D_A, D_B, D_B, D_B, D_C, D_C, D_C, D_C, N_BRANCH * D_MODEL)
D_IN = sum(WIDTHS)

kernel_name = "hybrid_conformer_gmlp_shortconv_step"


def rms_norm(x, g):
    xf = x.astype(jnp.float32)
    y = xf * lax.rsqrt(jnp.mean(xf * xf, axis=-1, keepdims=True) + RMS_EPS)
    return (y * g.astype(jnp.float32)).astype(x.dtype)


def layer_norm(x, g, b):
    xf = x.astype(jnp.float32)
    mu = jnp.mean(xf, axis=-1, keepdims=True)
    var = jnp.mean(jnp.square(xf - mu), axis=-1, keepdims=True)
    y = (xf - mu) * lax.rsqrt(var + LN_EPS)
    return (y * g.astype(jnp.float32) + b.astype(jnp.float32)).astype(x.dtype)


def causal_dwconv(x, past, w):
    k = w.shape[0]
    xp = jnp.concatenate([past.astype(x.dtype), x], axis=1)
    y = lax.conv_general_dilated(
        xp, w[:, None, :].astype(x.dtype), window_strides=(1,), padding='VALID',
        dimension_numbers=('NWC', 'WIO', 'NWC'), feature_group_count=x.shape[-1])
    return y, xp[:, -(k - 1):]


def spatial_gate(u, v, w_s, b_s):
    bsz, seq_len, _ = v.shape
    mask = jnp.tril(jnp.ones((CHUNK, CHUNK), dtype=bool))
    w = jnp.where(mask, w_s, jnp.zeros_like(w_s))
    n_full, rem = seq_len // CHUNK, seq_len % CHUNK
    parts = []
    if n_full:
        vf = v[:, :n_full * CHUNK].reshape(bsz, n_full, CHUNK, N_GROUPS_B, D_GROUP_B)
        s = jnp.einsum('gij,bcjgd->bcigd', w, vf) + b_s.T[None, None, :, :, None]
        parts.append(s.reshape(bsz, n_full * CHUNK, D_B))
    if rem:
        vr = v[:, n_full * CHUNK:].reshape(bsz, rem, N_GROUPS_B, D_GROUP_B)
        s = jnp.einsum('gij,bjgd->bigd', w[:, :rem, :rem], vr) + b_s[:, :rem].T[None, :, :, None]
        parts.append(s.reshape(bsz, rem, D_B))
    s = parts[0] if len(parts) == 1 else jnp.concatenate(parts, axis=1)
    return u * s.astype(u.dtype)


def mixer_layer(x, conv_a_past, conv_c_past, lw):
    (norm_g, w_in, b_gate, w_conv_a, b_conv_a, ln_a_g, ln_a_b, w_proj_a,
     ln_b_g, ln_b_b, w_s, b_s, w_proj_b, w_conv_c, w_proj_c, w_out) = lw
    xn = rms_norm(x, norm_g)
    h = jnp.einsum('bld,de->ble', xn, w_in)
    a_val, a_gate, z_a, u, v, z_b, gb, gc, hc, z_c, gates = jnp.split(
        h, [int(i) for i in np.cumsum(WIDTHS)[:-1]], axis=-1)
    a = a_val * jax.nn.sigmoid(a_gate)
    a_conv, new_a = causal_dwconv(a, conv_a_past, w_conv_a)
    a_out = jax.nn.silu(layer_norm(a_conv + b_conv_a, ln_a_g, ln_a_b)) * jax.nn.silu(z_a)
    p_a = jnp.einsum('blc,cd->bld', a_out, w_proj_a)
    vn = layer_norm(v, ln_b_g, ln_b_b)
    s = spatial_gate(u, vn, w_s, b_s)
    p_b = jnp.einsum('blc,cd->bld', s * jax.nn.silu(z_b), w_proj_b)
    ci = gc * hc
    c_conv, new_c = causal_dwconv(ci, conv_c_past, w_conv_c)
    p_c = jnp.einsum('blc,cd->bld', gb * c_conv * jax.nn.silu(z_c), w_proj_c)
    g = jax.nn.sigmoid(gates + b_gate).reshape(gates.shape[:-1] + (N_BRANCH, D_MODEL))
    m = g[..., 0, :] * p_a + g[..., 1, :] * p_b + g[..., 2, :] * p_c
    y = x + jnp.einsum('bld,de->ble', m, w_out)
    return y, new_a, new_c, vn


def setup_inputs(seed: int = 0) -> dict:
    key = jax.random.key(seed)
    ks = jax.random.split(key, 24)
    f32 = jnp.float32
    nrm = lambda k, shape, scale: jax.random.normal(k, shape, f32) * scale
    return {
        "x_prompt": nrm(ks[0], (BATCH, SEQ, D_MODEL), 1.0),
        "x_sample": nrm(ks[1], (DEC_BATCH, DEC_SEQ, D_MODEL), 1.0),
        "state_conv_a": nrm(ks[2], (DEPTH, DEC_BATCH, CONV_A_WIDTH - 1, D_A), 0.5),
        "state_conv_c": nrm(ks[3], (DEPTH, DEC_BATCH, CONV_C_WIDTH - 1, D_C), 0.5),
        "norm_g": 1.0 + nrm(ks[4], (DEPTH, D_MODEL), 0.02),
        "w_in": nrm(ks[5], (DEPTH, D_MODEL, D_IN), D_MODEL ** -0.5),
        "b_gate": nrm(ks[6], (DEPTH, N_BRANCH * D_MODEL), 0.02),
        "w_conv_a": nrm(ks[7], (DEPTH, CONV_A_WIDTH, D_A), CONV_A_WIDTH ** -0.5),
        "b_conv_a": nrm(ks[8], (DEPTH, D_A), 0.02),
        "ln_a_g": 1.0 + nrm(ks[9], (DEPTH, D_A), 0.02),
        "ln_a_b": nrm(ks[10], (DEPTH, D_A), 0.02),
        "w_proj_a": nrm(ks[11], (DEPTH, D_A, D_MODEL), D_A ** -0.5),
        "ln_b_g": 1.0 + nrm(ks[12], (DEPTH, D_B), 0.02),
        "ln_b_b": nrm(ks[13], (DEPTH, D_B), 0.02),
        "w_s": nrm(ks[14], (DEPTH, N_GROUPS_B, CHUNK, CHUNK), CHUNK ** -0.5),
        "b_s": 1.0 + nrm(ks[15], (DEPTH, N_GROUPS_B, CHUNK), 0.1),
        "w_proj_b": nrm(ks[16], (DEPTH, D_B, D_MODEL), D_B ** -0.5),
        "w_conv_c": nrm(ks[17], (DEPTH, CONV_C_WIDTH, D_C), CONV_C_WIDTH ** -0.5),
        "w_proj_c": nrm(ks[18], (DEPTH, D_C, D_MODEL), D_C ** -0.5),
        "w_out": nrm(ks[19], (DEPTH, D_MODEL, D_MODEL), 0.5 * D_MODEL ** -0.5),
        "final_g": 1.0 + nrm(ks[20], (D_MODEL,), 0.02),
    }


def reference(x_prompt, x_sample, state_conv_a, state_conv_c, norm_g, w_in, b_gate,
              w_conv_a, b_conv_a, ln_a_g, ln_a_b, w_proj_a, ln_b_g, ln_b_b, w_s, b_s,
              w_proj_b, w_conv_c, w_proj_c, w_out, final_g):
    xp, xs = x_prompt, x_sample
    bp = x_prompt.shape[0]
    a_p, a_s, c_p, c_s, v_s = [], [], [], [], []
    for l in range(DEPTH):
        lw = (norm_g[l], w_in[l], b_gate[l], w_conv_a[l], b_conv_a[l], ln_a_g[l], ln_a_b[l],
              w_proj_a[l], ln_b_g[l], ln_b_b[l], w_s[l], b_s[l], w_proj_b[l], w_conv_c[l],
              w_proj_c[l], w_out[l])
        zero_a = jnp.zeros((bp, CONV_A_WIDTH - 1, D_A), xp.dtype)
        zero_c = jnp.zeros((bp, CONV_C_WIDTH - 1, D_C), xp.dtype)
        xp, na_p, nc_p, _ = mixer_layer(xp, zero_a, zero_c, lw)
        xs, na_s, nc_s, vn_s = mixer_layer(xs, state_conv_a[l], state_conv_c[l], lw)
        a_p.append(na_p); a_s.append(na_s); c_p.append(nc_p); c_s.append(nc_s); v_s.append(vn_s)
    y_prompt = rms_norm(xp, final_g)
    y_sample = rms_norm(xs, final_g)
    return (y_prompt, y_sample, jnp.stack(a_p), jnp.stack(a_s), jnp.stack(c_p), jnp.stack(c_s), jnp.stack(v_s))
```

```python
import functools

import jax
import jax.numpy as jnp
from jax import lax
from jax.experimental import pallas as pl
from jax.experimental.pallas import tpu as pltpu

D = 1024
DEPTH = 4
KA = 31
KC = 3
CHUNK = 128
NG = 8
DG = D // NG
RMS_EPS = 1e-6
LN_EPS = 1e-5

COL_A, COL_B, COL_C, COL_G = 0, 3, 6, 10
N_COLS = 13

T_TILE = 256
RB = 16
CONV_RB = 32
SUBLANES = 8
LANES = 128
HALO = 32
VMEM_LIMIT = 60 * 1024 * 1024

F32 = jnp.float32
BF16 = jnp.bfloat16


def _sigmoid(x):
    return jax.nn.sigmoid(x)


def _silu(x):
    return x * jax.nn.sigmoid(x)


def _layer_norm(x, g, b):
    mu = jnp.mean(x, axis=-1, keepdims=True)
    xc = x - mu
    var = jnp.mean(xc * xc, axis=-1, keepdims=True)
    return xc * lax.rsqrt(var + LN_EPS) * g + b


def _rms_norm(x, g):
    return x * lax.rsqrt(jnp.mean(x * x, axis=-1, keepdims=True) + RMS_EPS) * g


def _rows(r, n=RB):
    return pl.ds(pl.multiple_of(r * n, n), n)


def _dot(a, b):
    return jnp.dot(a, b, preferred_element_type=F32)


def _causal_conv_block(buf, w_ref, r, cols, ntaps, front):
    r0 = pl.multiple_of(r * CONV_RB, CONV_RB)
    y = None
    for s in range(SUBLANES):
        taps = [k for k in range(ntaps) if (front - (ntaps - 1) + k) % SUBLANES == s]
        if not taps:
            continue
        n = CONV_RB if s == 0 else CONV_RB + SUBLANES
        z = None
        for k in taps:
            q0 = front - (ntaps - 1) + k - s
            t = buf[pl.ds(r0 + q0, n), cols] * w_ref[k:k + 1, cols]
            z = t if z is None else z + t
        z = z[s:s + CONV_RB]
        y = z if y is None else y + z
    return y


def _prompt_kernel(x_ref, ng_ref, win_ref, bg_ref, wca_ref, bca_ref, lag_ref, lab_ref, wpa_ref,
                   lbg_ref, lbb_ref, ws_ref, bsx_ref, wpb_ref, wcc_ref, wpc_ref, wout_ref, fg_ref,
                   y_ref, na_ref, nc_ref,
                   xn_s, h_s, abuf, cbuf, act_s, m_s, *, final):
    i = pl.program_id(1)
    n_i = pl.num_programs(1)
    T = T_TILE
    nrb = T // RB

    @pl.when(i == 0)
    def _():
        abuf[0:HALO, :] = jnp.zeros((HALO, D), F32)
        cbuf[0:8, :] = jnp.zeros((8, D), F32)

    def norm_body(r, c):
        rows = _rows(r)
        xn_s[rows, :] = _rms_norm(x_ref[0, rows, :], ng_ref[...]).astype(BF16)
        return c
    lax.fori_loop(0, nrb, norm_body, 0)

    def gate_proj(col, w_proj_ref, first):
        h_s[:, 0:D] = _dot(xn_s[...], win_ref[:, (COL_G + col) * D:(COL_G + col + 1) * D])
        h_s[:, D:2 * D] = _dot(act_s[...], w_proj_ref[...])

        def body(r, c):
            rows = _rows(r)
            g = _sigmoid(h_s[rows, 0:D] + bg_ref[:, col * D:(col + 1) * D])
            v = g * h_s[rows, D:2 * D]
            if first:
                m_s[rows, :] = v
            else:
                m_s[rows, :] += v
            return c
        lax.fori_loop(0, nrb, body, 0)

    h_s[:, 0:3 * D] = _dot(xn_s[...], win_ref[:, COL_A * D:(COL_A + 3) * D])

    def glu_body(r, c):
        rows = _rows(r)
        abuf[pl.ds(pl.multiple_of(r * RB, RB) + HALO, RB), :] = (
            h_s[rows, 0:D] * _sigmoid(h_s[rows, D:2 * D]))
        return c
    lax.fori_loop(0, nrb, glu_body, 0)

    def conv_a_body(r, c):
        rows = _rows(r, CONV_RB)
        for cb in range(D // LANES):
            cols = slice(cb * LANES, (cb + 1) * LANES)
            h_s[rows, cols] = _causal_conv_block(abuf, wca_ref, r, cols, KA, HALO)
        return c
    lax.fori_loop(0, T // CONV_RB, conv_a_body, 0)

    def a_body(r, c):
        rows = _rows(r)
        ln = _layer_norm(h_s[rows, 0:D] + bca_ref[...], lag_ref[...], lab_ref[...])
        act_s[rows, :] = (_silu(ln) * _silu(h_s[rows, 2 * D:3 * D])).astype(BF16)
        return c
    lax.fori_loop(0, nrb, a_body, 0)

    @pl.when(i == n_i - 1)
    def _():
        na_ref[0] = abuf[T + HALO - (KA - 1):T + HALO, :]
    abuf[0:HALO, :] = abuf[T:T + HALO, :]
    gate_proj(0, wpa_ref, True)

    h_s[:, 0:3 * D] = _dot(xn_s[...], win_ref[:, COL_B * D:(COL_B + 3) * D])

    def vn_body(r, c):
        rows = _rows(r)
        act_s[rows, :] = _layer_norm(h_s[rows, D:2 * D], lbg_ref[...], lbb_ref[...]).astype(BF16)
        return c
    lax.fori_loop(0, nrb, vn_body, 0)

    tri = (lax.broadcasted_iota(jnp.int32, (CHUNK, CHUNK), 0)
           >= lax.broadcasted_iota(jnp.int32, (CHUNK, CHUNK), 1))
    for g in range(NG):
        w_g = jnp.where(tri, ws_ref[g], jnp.zeros((CHUNK, CHUNK), BF16))
        for ch in range(T // CHUNK):
            r0, c0 = ch * CHUNK, g * DG
            s = _dot(w_g, act_s[r0:r0 + CHUNK, c0:c0 + DG]) + bsx_ref[:, c0:c0 + DG]
            h_s[r0:r0 + CHUNK, D + c0:D + c0 + DG] = s

    def b_body(r, c):
        rows = _rows(r)
        act_s[rows, :] = (h_s[rows, 0:D] * h_s[rows, D:2 * D]
                          * _silu(h_s[rows, 2 * D:3 * D])).astype(BF16)
        return c
    lax.fori_loop(0, nrb, b_body, 0)
    gate_proj(1, wpb_ref, False)

    h_s[:, 0:4 * D] = _dot(xn_s[...], win_ref[:, COL_C * D:(COL_C + 4) * D])

    def ci_body(r, c):
        rows = _rows(r)
        cbuf[pl.ds(pl.multiple_of(r * RB, RB) + 8, RB), :] = (
            h_s[rows, D:2 * D] * h_s[rows, 2 * D:3 * D])
        return c
    lax.fori_loop(0, nrb, ci_body, 0)

    def conv_c_body(r, c):
        rows = _rows(r, CONV_RB)
        for cb in range(D // LANES):
            cols = slice(cb * LANES, (cb + 1) * LANES)
            h_s[rows, D + cb * LANES:D + (cb + 1) * LANES] = _causal_conv_block(
                cbuf, wcc_ref, r, cols, KC, 8)
        return c
    lax.fori_loop(0, T // CONV_RB, conv_c_body, 0)

    def c_body(r, c):
        rows = _rows(r)
        act_s[rows, :] = (h_s[rows, 0:D] * h_s[rows, D:2 * D]
                          * _silu(h_s[rows, 3 * D:4 * D])).astype(BF16)
        return c
    lax.fori_loop(0, nrb, c_body, 0)

    @pl.when(i == n_i - 1)
    def _():
        nc_ref[0] = cbuf[T + 8 - (KC - 1):T + 8, :]
    cbuf[0:8, :] = cbuf[T:T + 8, :]
    gate_proj(2, wpc_ref, False)

    h_s[:, 0:D] = _dot(m_s[...].astype(BF16), wout_ref[...])

    def out_body(r, c):
        rows = _rows(r)
        y = x_ref[0, rows, :] + h_s[rows, 0:D]
        if final:
            y = _rms_norm(y, fg_ref[...])
        y_ref[0, rows, :] = y
        return c
    lax.fori_loop(0, nrb, out_body, 0)


def _resident(shape, layer):
    nd = len(shape)
    return pl.BlockSpec((None,) + tuple(shape[1:]),
                        lambda *_: (layer,) + (0,) * (nd - 1),
                        pipeline_mode=pl.Buffered(1))


def _prompt_layer(x, layer, final, p):
    bsz, seq, _ = x.shape
    n_t = seq // T_TILE
    weights = (p["norm_g"], p["w_in"], p["b_gate"], p["w_conv_a"], p["b_conv_a"], p["ln_a_g"],
               p["ln_a_b"], p["w_proj_a"], p["ln_b_g"], p["ln_b_b"], p["w_s"], p["bsx"],
               p["w_proj_b"], p["w_conv_c"], p["w_proj_c"], p["w_out"])
    in_specs = [pl.BlockSpec((1, T_TILE, D), lambda b, i: (b, i, 0))]
    in_specs += [_resident(w.shape, layer) for w in weights]
    in_specs += [pl.BlockSpec((1, D), lambda b, i: (0, 0), pipeline_mode=pl.Buffered(1))]
    out_shape = (jax.ShapeDtypeStruct((bsz, seq, D), F32),
                 jax.ShapeDtypeStruct((bsz, KA - 1, D), F32),
                 jax.ShapeDtypeStruct((bsz, KC - 1, D), F32))
    out_specs = (pl.BlockSpec((1, T_TILE, D), lambda b, i: (b, i, 0)),
                 pl.BlockSpec((1, KA - 1, D), lambda b, i: (b, 0, 0)),
                 pl.BlockSpec((1, KC - 1, D), lambda b, i: (b, 0, 0)))
    scratch = [pltpu.VMEM((T_TILE, D), BF16),
               pltpu.VMEM((T_TILE, 4 * D), F32),
               pltpu.VMEM((HALO + T_TILE, D), F32),
               pltpu.VMEM((8 + T_TILE, D), F32),
               pltpu.VMEM((T_TILE, D), BF16),
               pltpu.VMEM((T_TILE, D), F32)]
    return pl.pallas_call(
        functools.partial(_prompt_kernel, final=final),
        grid=(bsz, n_t),
        in_specs=in_specs, out_specs=out_specs, out_shape=out_shape,
        scratch_shapes=scratch,
        compiler_params=pltpu.CompilerParams(
            dimension_semantics=("arbitrary", "arbitrary"), vmem_limit_bytes=VMEM_LIMIT),
        name=f"prompt_layer{layer}",
    )(x, *weights, p["final_g"])


BS = 32
NS = 4


def _sample_kernel(x_ref, sa_ref, sc_ref, ng_ref, win_ref, bg_ref, wca_ref, bca_ref, lag_ref,
                   lab_ref, wpa_ref, lbg_ref, lbb_ref, wsx_ref, bsx_ref, wpb_ref, wcc_ref,
                   wpc_ref, wout_ref, fg_ref,
                   y_ref, na_ref, nc_ref, nv_ref,
                   xn_s, act_s, *, final):
    M = NS * BS
    x = x_ref[...].reshape(M, D)
    xn_s[...] = _rms_norm(x, ng_ref[...]).astype(BF16)

    def proj(col, n=1):
        return _dot(xn_s[...], win_ref[:, col * D:(col + n) * D])

    def gate(col):
        return _sigmoid(proj(COL_G + col) + bg_ref[:, col * D:(col + 1) * D])

    h = proj(COL_A, 3)
    a = h[:, 0:D] * _sigmoid(h[:, D:2 * D])
    a_t = [a[t * BS:(t + 1) * BS, :] for t in range(NS)]

    def hist_a(j):
        return sa_ref[j] if j < KA - 1 else a_t[j - (KA - 1)]
    conv = []
    for t in range(NS):
        acc = hist_a(t) * wca_ref[0:1, :]
        for k in range(1, KA):
            acc = acc + hist_a(t + k) * wca_ref[k:k + 1, :]
        conv.append(acc)
    conv = jnp.concatenate(conv, axis=0)
    ln = _layer_norm(conv + bca_ref[...], lag_ref[...], lab_ref[...])
    act_s[...] = (_silu(ln) * _silu(h[:, 2 * D:3 * D])).astype(BF16)
    for j in range(KA - 1):
        na_ref[j] = hist_a(j + NS)
    m = gate(0) * _dot(act_s[...], wpa_ref[...])

    h = proj(COL_B, 3)
    vn = _layer_norm(h[:, D:2 * D], lbg_ref[...], lbb_ref[...])
    nv_ref[...] = vn.reshape(NS, BS, D)
    s = []
    for t in range(NS):
        acc = bsx_ref[t:t + 1, :]
        for j in range(t + 1):
            acc = acc + wsx_ref[t, j:j + 1, :] * vn[j * BS:(j + 1) * BS, :]
        s.append(acc)
    s = jnp.concatenate(s, axis=0)
    act_s[...] = (h[:, 0:D] * s * _silu(h[:, 2 * D:3 * D])).astype(BF16)
    m = m + gate(1) * _dot(act_s[...], wpb_ref[...])

    h = proj(COL_C, 4)
    ci = h[:, D:2 * D] * h[:, 2 * D:3 * D]
    ci_t = [ci[t * BS:(t + 1) * BS, :] for t in range(NS)]

    def hist_c(j):
        return sc_ref[j] if j < KC - 1 else ci_t[j - (KC - 1)]
    conv = []
    for t in range(NS):
        acc = hist_c(t) * wcc_ref[0:1, :]
        for k in range(1, KC):
            acc = acc + hist_c(t + k) * wcc_ref[k:k + 1, :]
        conv.append(acc)
    conv = jnp.concatenate(conv, axis=0)
    act_s[...] = (h[:, 0:D] * conv * _silu(h[:, 3 * D:4 * D])).astype(BF16)
    for j in range(KC - 1):
        nc_ref[j] = hist_c(j + NS)
    m = m + gate(2) * _dot(act_s[...], wpc_ref[...])

    y = x + _dot(m.astype(BF16), wout_ref[...])
    if final:
        y = _rms_norm(y, fg_ref[...])
    y_ref[...] = y.reshape(NS, BS, D)


def _sample_layer(x, sa, sc, layer, final, p):
    n_b = x.shape[1]
    weights = (p["norm_g"], p["w_in"], p["b_gate"], p["w_conv_a"], p["b_conv_a"], p["ln_a_g"],
               p["ln_a_b"], p["w_proj_a"], p["ln_b_g"], p["ln_b_b"], p["wsx"], p["bsx"],
               p["w_proj_b"], p["w_conv_c"], p["w_proj_c"], p["w_out"])

    def seq_block(rows):
        return pl.BlockSpec((rows, BS, D), lambda b: (0, b, 0))

    def state_block(rows):
        return pl.BlockSpec((None, rows, BS, D), lambda b: (layer, 0, b, 0))

    in_specs = [seq_block(NS), state_block(KA - 1), state_block(KC - 1)]
    in_specs += [_resident(w.shape, layer) for w in weights]
    in_specs += [pl.BlockSpec((1, D), lambda b: (0, 0), pipeline_mode=pl.Buffered(1))]
    out_shape = (jax.ShapeDtypeStruct((NS, n_b, D), F32),
                 jax.ShapeDtypeStruct((KA - 1, n_b, D), F32),
                 jax.ShapeDtypeStruct((KC - 1, n_b, D), F32),
                 jax.ShapeDtypeStruct((NS, n_b, D), F32))
    out_specs = (seq_block(NS), seq_block(KA - 1), seq_block(KC - 1), seq_block(NS))
    scratch = [pltpu.VMEM((NS * BS, D), BF16), pltpu.VMEM((NS * BS, D), BF16)]
    return pl.pallas_call(
        functools.partial(_sample_kernel, final=final),
        grid=(n_b // BS,),
        in_specs=in_specs, out_specs=out_specs, out_shape=out_shape,
        scratch_shapes=scratch,
        compiler_params=pltpu.CompilerParams(
            dimension_semantics=("arbitrary",), vmem_limit_bytes=VMEM_LIMIT),
        name=f"sample_layer{layer}",
    )(x, sa, sc, *weights, p["final_g"])


def kernel(x_prompt, x_sample, state_conv_a, state_conv_c, norm_g, w_in, b_gate, w_conv_a, b_conv_a, ln_a_g, ln_a_b, w_proj_a, ln_b_g, ln_b_b, w_s, b_s, w_proj_b, w_conv_c, w_proj_c, w_out, final_g):
    row = lambda v: v[:, None, :]
    p = {
        "norm_g": row(norm_g), "w_in": w_in.astype(BF16), "b_gate": row(b_gate),
        "w_conv_a": w_conv_a, "b_conv_a": row(b_conv_a), "ln_a_g": row(ln_a_g),
        "ln_a_b": row(ln_a_b), "w_proj_a": w_proj_a.astype(BF16), "ln_b_g": row(ln_b_g),
        "ln_b_b": row(ln_b_b), "w_s": w_s.astype(BF16),
        "bsx": jnp.repeat(jnp.swapaxes(b_s, 1, 2), DG, axis=2),
        "wsx": jnp.repeat(jnp.transpose(
            jnp.where(jnp.tril(jnp.ones((NS, NS), bool)), w_s[:, :, :NS, :NS], 0.0),
            (0, 2, 3, 1)), DG, axis=3),
        "w_proj_b": w_proj_b.astype(BF16), "w_conv_c": w_conv_c,
        "w_proj_c": w_proj_c.astype(BF16), "w_out": w_out.astype(BF16),
        "final_g": final_g[None, :],
    }
    xs = jnp.swapaxes(x_sample, 0, 1)
    sa = jnp.swapaxes(state_conv_a, 1, 2)
    sc = jnp.swapaxes(state_conv_c, 1, 2)

    xp = x_prompt
    a_p, a_s, c_p, c_s, v_s = [], [], [], [], []
    for layer in range(DEPTH):
        final = layer == DEPTH - 1
        xp, na_p, nc_p = _prompt_layer(xp, layer, final, p)
        xs, na_s, nc_s, nv_s = _sample_layer(xs, sa, sc, layer, final, p)
        a_p.append(na_p); a_s.append(na_s); c_p.append(nc_p); c_s.append(nc_s); v_s.append(nv_s)
    to_seq_major = lambda v: jnp.swapaxes(jnp.stack(v), 1, 2)
    return (xp, jnp.swapaxes(xs, 0, 1), jnp.stack(a_p), to_seq_major(a_s), jnp.stack(c_p),
            to_seq_major(c_s), to_seq_major(v_s))
```

```python
import functools

import jax
import jax.numpy as jnp
from jax import lax
from jax.experimental import pallas as pl
from jax.experimental.pallas import tpu as pltpu

D = 1024
DEPTH = 4
KA = 31
KC = 3
CHUNK = 128
NG = 8
DG = D // NG
RMS_EPS = 1e-6
LN_EPS = 1e-5

COL_A, COL_B, COL_C, COL_G = 0, 3, 6, 10

T_TILE = 256
RB = 16
CONV_RB = 32
SUBLANES = 8
LANES = 128
HALO = 32
VMEM_LIMIT = 60 * 1024 * 1024

F32 = jnp.float32
BF16 = jnp.bfloat16


def _sigmoid(x):
    return jax.nn.sigmoid(x)


def _silu(x):
    return x * jax.nn.sigmoid(x)


def _layer_norm(x, g, b):
    mu = jnp.mean(x, axis=-1, keepdims=True)
    xc = x - mu
    var = jnp.mean(xc * xc, axis=-1, keepdims=True)
    return xc * lax.rsqrt(var + LN_EPS) * g + b


def _rms_norm(x, g):
    return x * lax.rsqrt(jnp.mean(x * x, axis=-1, keepdims=True) + RMS_EPS) * g


def _dot(a, b):
    return jnp.dot(a, b, preferred_element_type=F32)


def _causal_conv_block(buf, w_ref, r, cols, ntaps, front):
    r0 = r * CONV_RB
    y = None
    for s in range(SUBLANES):
        taps = [k for k in range(ntaps) if (front - (ntaps - 1) + k) % SUBLANES == s]
        if not taps:
            continue
        n = CONV_RB if s == 0 else CONV_RB + SUBLANES
        z = None
        for k in taps:
            q0 = r0 + front - (ntaps - 1) + k - s
            t = buf[q0:q0 + n, cols] * w_ref[k:k + 1, cols]
            z = t if z is None else z + t
        z = z[s:s + CONV_RB]
        y = z if y is None else y + z
    return y


def _interleave(mxu_ops, vpu_ops):
    done = 0
    for k, op in enumerate(mxu_ops):
        op()
        upto = (k + 1) * len(vpu_ops) // len(mxu_ops)
        for v in vpu_ops[done:upto]:
            v()
        done = upto


def _prompt_kernel(x_ref, ng_ref, win_ref, bg_ref, wca_ref, bca_ref, lag_ref, lab_ref, wpa_ref,
                   lbg_ref, lbb_ref, ws_ref, bsx_ref, wpb_ref, wcc_ref, wpc_ref, wout_ref, fg_ref,
                   y_ref, na_ref, nc_ref,
                   xn_s, ha_s, hb_s, hc_s, p_s, abuf, cbuf, vn_s, acta_s, actb_s, actc_s, m_s,
                   *, final):
    T = T_TILE
    blocks = range(T // RB)
    conv_blocks = range(T // CONV_RB)
    per_conv = CONV_RB // RB

    def rows(r, n=RB):
        return slice(r * n, (r + 1) * n)

    def col(j):
        return slice(j * D, (j + 1) * D)

    @pl.when(pl.program_id(1) == 0)
    def _():
        abuf[0:HALO, :] = jnp.zeros((HALO, D), F32)
        cbuf[0:SUBLANES, :] = jnp.zeros((SUBLANES, D), F32)

    def project(dst, dst_col, lhs, w_ref, w_col=None):
        def op():
            w = w_ref[...] if w_col is None else w_ref[:, col(w_col)]
            dst[:, col(dst_col)] = _dot(lhs[...], w)
        return op

    def norm(r):
        xn_s[rows(r), :] = _rms_norm(x_ref[0, rows(r), :], ng_ref[...]).astype(BF16)

    def glu(r):
        abuf[HALO + r * RB:HALO + (r + 1) * RB, :] = (
            ha_s[rows(r), col(0)] * _sigmoid(ha_s[rows(r), col(1)]))

    def conv_a(r):
        for cb in range(D // LANES):
            cols = slice(cb * LANES, (cb + 1) * LANES)
            ha_s[rows(r, CONV_RB), cols] = _causal_conv_block(abuf, wca_ref, r, cols, KA, HALO)

    def ln_a(r):
        ln = _layer_norm(ha_s[rows(r), col(0)] + bca_ref[...], lag_ref[...], lab_ref[...])
        acta_s[rows(r), :] = (_silu(ln) * _silu(ha_s[rows(r), col(2)])).astype(BF16)

    def ln_v(r):
        vn_s[rows(r), :] = _layer_norm(hb_s[rows(r), col(1)], lbg_ref[...], lbb_ref[...]).astype(BF16)

    def comb_b(r):
        actb_s[rows(r), :] = (hb_s[rows(r), col(0)] * hb_s[rows(r), col(1)]
                              * _silu(hb_s[rows(r), col(2)])).astype(BF16)

    def gate_in_c(r):
        cbuf[SUBLANES + r * RB:SUBLANES + (r + 1) * RB, :] = (
            hc_s[rows(r), col(1)] * hc_s[rows(r), col(2)])

    def conv_c(r):
        for cb in range(D // LANES):
            cols = slice(cb * LANES, (cb + 1) * LANES)
            hc_s[rows(r, CONV_RB), D + cb * LANES:D + (cb + 1) * LANES] = _causal_conv_block(
                cbuf, wcc_ref, r, cols, KC, SUBLANES)

    def comb_c(r):
        actc_s[rows(r), :] = (hc_s[rows(r), col(0)] * hc_s[rows(r), col(1)]
                              * _silu(hc_s[rows(r), col(3)])).astype(BF16)

    def merge(branch, first, last):
        def op(r):
            g = _sigmoid(ha_s[rows(r), col(branch)] + bg_ref[:, col(branch)])
            v = g * p_s[rows(r), col(branch)]
            if not first:
                v = m_s[rows(r), :] + v
            if last:
                actb_s[rows(r), :] = v.astype(BF16)
            else:
                m_s[rows(r), :] = v
        return op

    def residual(r):
        y = x_ref[0, rows(r), :] + p_s[rows(r), col(0)]
        if final:
            y = _rms_norm(y, fg_ref[...])
        y_ref[0, rows(r), :] = y

    def each(fn, rng=blocks):
        return [functools.partial(fn, r) for r in rng]

    def spatial_gate():
        tri = (lax.broadcasted_iota(jnp.int32, (CHUNK, CHUNK), 0)
               >= lax.broadcasted_iota(jnp.int32, (CHUNK, CHUNK), 1))
        ops = []
        for g in range(NG):
            def op(g=g):
                w_g = jnp.where(tri, ws_ref[g], jnp.zeros((CHUNK, CHUNK), BF16))
                for ch in range(T // CHUNK):
                    rws, cls = slice(ch * CHUNK, (ch + 1) * CHUNK), slice(g * DG, (g + 1) * DG)
                    hb_s[rws, D + g * DG:D + (g + 1) * DG] = (
                        _dot(w_g, vn_s[rws, cls]) + bsx_ref[:, cls])
            ops.append(op)
        return ops

    def branch_a_vector():
        ops = []
        for cbk in conv_blocks:
            ops += [functools.partial(glu, cbk * per_conv + j) for j in range(per_conv)]
            ops += [functools.partial(conv_a, cbk)]
            ops += [functools.partial(ln_a, cbk * per_conv + j) for j in range(per_conv)]
        return ops

    def branch_c_vector():
        ops = []
        for cbk in conv_blocks:
            ops += [functools.partial(gate_in_c, cbk * per_conv + j) for j in range(per_conv)]
            ops += [functools.partial(conv_c, cbk)]
            ops += [functools.partial(comb_c, cbk * per_conv + j) for j in range(per_conv)]
        return ops

    def carry_a():
        na_ref[0] = abuf[T + HALO - (KA - 1):T + HALO, :]
        abuf[0:HALO, :] = abuf[T:T + HALO, :]

    def carry_c():
        nc_ref[0] = cbuf[T + SUBLANES - (KC - 1):T + SUBLANES, :]
        cbuf[0:SUBLANES, :] = cbuf[T:T + SUBLANES, :]

    for op in each(norm):
        op()
    _interleave([project(ha_s, 1, xn_s, win_ref, COL_A + 1),
                 project(ha_s, 0, xn_s, win_ref, COL_A),
                 project(ha_s, 2, xn_s, win_ref, COL_A + 2)], [])
    _interleave([project(hb_s, 1, xn_s, win_ref, COL_B + 1),
                 project(hb_s, 0, xn_s, win_ref, COL_B),
                 project(hb_s, 2, xn_s, win_ref, COL_B + 2),
                 project(hc_s, 1, xn_s, win_ref, COL_C + 1),
                 project(hc_s, 2, xn_s, win_ref, COL_C + 2),
                 project(hc_s, 0, xn_s, win_ref, COL_C),
                 project(hc_s, 3, xn_s, win_ref, COL_C + 3)],
                branch_a_vector() + [carry_a])
    _interleave([project(p_s, 0, acta_s, wpa_ref),
                 project(ha_s, 0, xn_s, win_ref, COL_G)],
                each(ln_v) + branch_c_vector() + [carry_c])
    _interleave(spatial_gate() + [project(p_s, 2, actc_s, wpc_ref),
                                  project(ha_s, 2, xn_s, win_ref, COL_G + 2)],
                each(merge(0, True, False)))
    _interleave([project(ha_s, 1, xn_s, win_ref, COL_G + 1)],
                each(comb_b) + each(merge(2, False, False)))
    project(p_s, 1, actb_s, wpb_ref)()
    for op in each(merge(1, False, True)):
        op()
    project(p_s, 0, actb_s, wout_ref)()
    for op in each(residual):
        op()


def _resident(shape, layer):
    nd = len(shape)
    return pl.BlockSpec((None,) + tuple(shape[1:]),
                        lambda *_: (layer,) + (0,) * (nd - 1),
                        pipeline_mode=pl.Buffered(1))


def _prompt_layer(x, layer, final, p):
    bsz, seq, _ = x.shape
    n_t = seq // T_TILE
    weights = (p["norm_g"], p["w_in"], p["b_gate"], p["w_conv_a"], p["b_conv_a"], p["ln_a_g"],
               p["ln_a_b"], p["w_proj_a"], p["ln_b_g"], p["ln_b_b"], p["w_s"], p["bsx"],
               p["w_proj_b"], p["w_conv_c"], p["w_proj_c"], p["w_out"])
    in_specs = [pl.BlockSpec((1, T_TILE, D), lambda b, i: (b, i, 0))]
    in_specs += [_resident(w.shape, layer) for w in weights]
    in_specs += [pl.BlockSpec((1, D), lambda b, i: (0, 0), pipeline_mode=pl.Buffered(1))]
    out_shape = (jax.ShapeDtypeStruct((bsz, seq, D), F32),
                 jax.ShapeDtypeStruct((bsz, KA - 1, D), F32),
                 jax.ShapeDtypeStruct((bsz, KC - 1, D), F32))
    out_specs = (pl.BlockSpec((1, T_TILE, D), lambda b, i: (b, i, 0)),
                 pl.BlockSpec((1, KA - 1, D), lambda b, i: (b, 0, 0)),
                 pl.BlockSpec((1, KC - 1, D), lambda b, i: (b, 0, 0)))
    scratch = [pltpu.VMEM((T_TILE, D), BF16),
               pltpu.VMEM((T_TILE, 3 * D), F32),
               pltpu.VMEM((T_TILE, 3 * D), F32),
               pltpu.VMEM((T_TILE, 4 * D), F32),
               pltpu.VMEM((T_TILE, 3 * D), F32),
               pltpu.VMEM((HALO + T_TILE, D), F32),
               pltpu.VMEM((SUBLANES + T_TILE, D), F32),
               pltpu.VMEM((T_TILE, D), BF16),
               pltpu.VMEM((T_TILE, D), BF16),
               pltpu.VMEM((T_TILE, D), BF16),
               pltpu.VMEM((T_TILE, D), BF16),
               pltpu.VMEM((T_TILE, D), F32)]
    return pl.pallas_call(
        functools.partial(_prompt_kernel, final=final),
        grid=(bsz, n_t),
        in_specs=in_specs, out_specs=out_specs, out_shape=out_shape,
        scratch_shapes=scratch,
        compiler_params=pltpu.CompilerParams(
            dimension_semantics=("arbitrary", "arbitrary"), vmem_limit_bytes=VMEM_LIMIT),
        name=f"prompt_layer{layer}",
    )(x, *weights, p["final_g"])


BS = 32
NS = 4


def _sample_kernel(x_ref, sa_ref, sc_ref, ng_ref, win_ref, bg_ref, wca_ref, bca_ref, lag_ref,
                   lab_ref, wpa_ref, lbg_ref, lbb_ref, wsx_ref, bsx_ref, wpb_ref, wcc_ref,
                   wpc_ref, wout_ref, fg_ref,
                   y_ref, na_ref, nc_ref, nv_ref,
                   xn_s, act_s, *, final):
    M = NS * BS
    x = x_ref[...].reshape(M, D)
    xn_s[...] = _rms_norm(x, ng_ref[...]).astype(BF16)

    def proj(col, n=1):
        return _dot(xn_s[...], win_ref[:, col * D:(col + n) * D])

    def gate(col):
        return _sigmoid(proj(COL_G + col) + bg_ref[:, col * D:(col + 1) * D])

    h = proj(COL_A, 3)
    a = h[:, 0:D] * _sigmoid(h[:, D:2 * D])
    a_t = [a[t * BS:(t + 1) * BS, :] for t in range(NS)]

    def hist_a(j):
        return sa_ref[j] if j < KA - 1 else a_t[j - (KA - 1)]
    conv = []
    for t in range(NS):
        acc = hist_a(t) * wca_ref[0:1, :]
        for k in range(1, KA):
            acc = acc + hist_a(t + k) * wca_ref[k:k + 1, :]
        conv.append(acc)
    conv = jnp.concatenate(conv, axis=0)
    ln = _layer_norm(conv + bca_ref[...], lag_ref[...], lab_ref[...])
    act_s[...] = (_silu(ln) * _silu(h[:, 2 * D:3 * D])).astype(BF16)
    for j in range(KA - 1):
        na_ref[j] = hist_a(j + NS)
    m = gate(0) * _dot(act_s[...], wpa_ref[...])

    h = proj(COL_B, 3)
    vn = _layer_norm(h[:, D:2 * D], lbg_ref[...], lbb_ref[...])
    nv_ref[...] = vn.reshape(NS, BS, D)
    s = []
    for t in range(NS):
        acc = bsx_ref[t:t + 1, :]
        for j in range(t + 1):
            acc = acc + wsx_ref[t, j:j + 1, :] * vn[j * BS:(j + 1) * BS, :]
        s.append(acc)
    s = jnp.concatenate(s, axis=0)
    act_s[...] = (h[:, 0:D] * s * _silu(h[:, 2 * D:3 * D])).astype(BF16)
    m = m + gate(1) * _dot(act_s[...], wpb_ref[...])

    h = proj(COL_C, 4)
    ci = h[:, D:2 * D] * h[:, 2 * D:3 * D]
    ci_t = [ci[t * BS:(t + 1) * BS, :] for t in range(NS)]

    def hist_c(j):
        return sc_ref[j] if j < KC - 1 else ci_t[j - (KC - 1)]
    conv = []
    for t in range(NS):
        acc = hist_c(t) * wcc_ref[0:1, :]
        for k in range(1, KC):
            acc = acc + hist_c(t + k) * wcc_ref[k:k + 1, :]
        conv.append(acc)
    conv = jnp.concatenate(conv, axis=0)
    act_s[...] = (h[:, 0:D] * conv * _silu(h[:, 3 * D:4 * D])).astype(BF16)
    for j in range(KC - 1):
        nc_ref[j] = hist_c(j + NS)
    m = m + gate(2) * _dot(act_s[...], wpc_ref[...])

    y = x + _dot(m.astype(BF16), wout_ref[...])
    if final:
        y = _rms_norm(y, fg_ref[...])
    y_ref[...] = y.reshape(NS, BS, D)


def _sample_layer(x, sa, sc, layer, final, p):
    n_b = x.shape[1]
    weights = (p["norm_g"], p["w_in"], p["b_gate"], p["w_conv_a"], p["b_conv_a"], p["ln_a_g"],
               p["ln_a_b"], p["w_proj_a"], p["ln_b_g"], p["ln_b_b"], p["wsx"], p["bsx"],
               p["w_proj_b"], p["w_conv_c"], p["w_proj_c"], p["w_out"])

    def seq_block(rows):
        return pl.BlockSpec((rows, BS, D), lambda b: (0, b, 0))

    def state_block(rows):
        return pl.BlockSpec((None, rows, BS, D), lambda b: (layer, 0, b, 0))

    in_specs = [seq_block(NS), state_block(KA - 1), state_block(KC - 1)]
    in_specs += [_resident(w.shape, layer) for w in weights]
    in_specs += [pl.BlockSpec((1, D), lambda b: (0, 0), pipeline_mode=pl.Buffered(1))]
    out_shape = (jax.ShapeDtypeStruct((NS, n_b, D), F32),
                 jax.ShapeDtypeStruct((KA - 1, n_b, D), F32),
                 jax.ShapeDtypeStruct((KC - 1, n_b, D), F32),
                 jax.ShapeDtypeStruct((NS, n_b, D), F32))
    out_specs = (seq_block(NS), seq_block(KA - 1), seq_block(KC - 1), seq_block(NS))
    scratch = [pltpu.VMEM((NS * BS, D), BF16), pltpu.VMEM((NS * BS, D), BF16)]
    return pl.pallas_call(
        functools.partial(_sample_kernel, final=final),
        grid=(n_b // BS,),
        in_specs=in_specs, out_specs=out_specs, out_shape=out_shape,
        scratch_shapes=scratch,
        compiler_params=pltpu.CompilerParams(
            dimension_semantics=("arbitrary",), vmem_limit_bytes=VMEM_LIMIT),
        name=f"sample_layer{layer}",
    )(x, sa, sc, *weights, p["final_g"])


def kernel(x_prompt, x_sample, state_conv_a, state_conv_c, norm_g, w_in, b_gate, w_conv_a, b_conv_a, ln_a_g, ln_a_b, w_proj_a, ln_b_g, ln_b_b, w_s, b_s, w_proj_b, w_conv_c, w_proj_c, w_out, final_g):
    row = lambda v: v[:, None, :]
    p = {
        "norm_g": row(norm_g), "w_in": w_in.astype(BF16), "b_gate": row(b_gate),
        "w_conv_a": w_conv_a, "b_conv_a": row(b_conv_a), "ln_a_g": row(ln_a_g),
        "ln_a_b": row(ln_a_b), "w_proj_a": w_proj_a.astype(BF16), "ln_b_g": row(ln_b_g),
        "ln_b_b": row(ln_b_b), "w_s": w_s.astype(BF16),
        "bsx": jnp.repeat(jnp.swapaxes(b_s, 1, 2), DG, axis=2),
        "wsx": jnp.repeat(jnp.transpose(
            jnp.where(jnp.tril(jnp.ones((NS, NS), bool)), w_s[:, :, :NS, :NS], 0.0),
            (0, 2, 3, 1)), DG, axis=3),
        "w_proj_b": w_proj_b.astype(BF16), "w_conv_c": w_conv_c,
        "w_proj_c": w_proj_c.astype(BF16), "w_out": w_out.astype(BF16),
        "final_g": final_g[None, :],
    }
    xs = jnp.swapaxes(x_sample, 0, 1)
    sa = jnp.swapaxes(state_conv_a, 1, 2)
    sc = jnp.swapaxes(state_conv_c, 1, 2)

    xp = x_prompt
    a_p, a_s, c_p, c_s, v_s = [], [], [], [], []
    for layer in range(DEPTH):
        final = layer == DEPTH - 1
        xp, na_p, nc_p = _prompt_layer(xp, layer, final, p)
        xs, na_s, nc_s, nv_s = _sample_layer(xs, sa, sc, layer, final, p)
        a_p.append(na_p); a_s.append(na_s); c_p.append(nc_p); c_s.append(nc_s); v_s.append(nv_s)
    to_seq_major = lambda v: jnp.swapaxes(jnp.stack(v), 1, 2)
    return (xp, jnp.swapaxes(xs, 0, 1), jnp.stack(a_p), to_seq_major(a_s), jnp.stack(c_p),
            to_seq_major(c_s), to_seq_major(v_s))
```
